```python
import math
import jax, jax.numpy as jnp
from jax import lax
import numpy as np

D_MODEL = 2048
BATCH = 2
SEQ = 4096
DEPTH = 2
DEC_BATCH = 8
DEC_SEQ = 8
PAST_LEN = 16384
PAGE_SIZE = 128

H_A = 4
DQK_A = 256
DV_A = 512
W_QK_A = H_A * DQK_A
W_V_A = H_A * DV_A
CONV_W = 4
MLSTM_CHUNK = 64
H_B = 8
D_QK_B = 128
DV_B = 2 * D_QK_B
W_QK_B = H_B * 2 * D_QK_B
W_V_B = H_B * DV_B
ROT_DIM = D_QK_B // 4
ROPE_THETA = 500000.0
Q_BLOCK = 128
D_FF = -(-8 * D_MODEL // (3 * 256)) * 256
N_ADA = 6
EPS = 1e-6
SPLITS = (2 * W_QK_A, W_V_A, W_V_A, H_A, H_A, W_QK_B, W_QK_B, W_V_B, D_MODEL, D_MODEL)
N_IN = sum(SPLITS)

kernel_name = "hybrid_mlstm_diffattn_adaln_step"


def rmsnorm(x, g):
    xf = x.astype(jnp.float32)
    y = xf * lax.rsqrt(jnp.mean(xf * xf, axis=-1, keepdims=True) + EPS)
    return (y * g.astype(jnp.float32)).astype(x.dtype)


def rope_partial(x, pos):
    inv = jnp.power(ROPE_THETA, -jnp.arange(0, ROT_DIM, 2, dtype=jnp.float32) / ROT_DIM)
    ang = pos.astype(jnp.float32)[:, None] * inv[None, :]
    cos = jnp.cos(ang)[None, :, None, None, :]
    sin = jnp.sin(ang)[None, :, None, None, :]
    xr = x[..., :ROT_DIM].astype(jnp.float32)
    x1, x2 = xr[..., :ROT_DIM // 2], xr[..., ROT_DIM // 2:]
    rot = jnp.concatenate([x1 * cos - x2 * sin, x2 * cos + x1 * sin], axis=-1).astype(x.dtype)
    return jnp.concatenate([rot, x[..., ROT_DIM:]], axis=-1)


def diff_attn_block(qi, qpos, k, v, kpos, lam):
    s = jnp.einsum('bqhmd,bkhmd->bhmqk', qi, k).astype(jnp.float32) * (D_QK_B ** -0.5)
    s = jnp.where(kpos[None, :] <= qpos[:, None], s, -jnp.inf)
    p = jax.nn.softmax(s, axis=-1)
    a = p[:, :, 0] - lam * p[:, :, 1]
    return jnp.einsum('bhqk,bkhv->bqhv', a.astype(v.dtype), v)


def mlstm_chunk(carry, inp):
    C, n, m = carry
    q, k, v, ig, lf = inp
    L = q.shape[1]
    b = jnp.cumsum(lf, axis=1).transpose(0, 2, 1)
    it = ig.transpose(0, 2, 1)
    causal = jnp.tril(jnp.ones((L, L), dtype=bool))
    log_d = jnp.where(causal, b[..., :, None] - b[..., None, :] + it[..., None, :], -jnp.inf)
    log_inter = b + m[..., None]
    m_t = jnp.maximum(log_inter, jnp.max(log_d, axis=-1))
    d = jnp.exp(log_d - m_t[..., None])
    w_inter = jnp.exp(log_inter - m_t)
    s = jnp.einsum('blhk,bshk->bhls', q, k) * d
    num = jnp.einsum('bhls,bshv->bhlv', s, v) + w_inter[..., None] * jnp.einsum('blhk,bhvk->bhlv', q, C)
    den = jnp.sum(s, axis=-1) + w_inter * jnp.einsum('blhk,bhk->bhl', q, n)
    h = num / jnp.maximum(jnp.abs(den), jnp.exp(-m_t))[..., None]
    decay = w_inter[..., -1]
    w_end = d[..., -1, :]
    C = decay[..., None, None] * C + jnp.einsum('bhs,bshv,bshk->bhvk', w_end, v, k)
    n = decay[..., None] * n + jnp.einsum('bhs,bshk->bhk', w_end, k)
    return (C, n, m_t[..., -1]), h.transpose(0, 2, 1, 3)


def run_mlstm(q, k, v, ig, lf, C, n, m):
    B, S = q.shape[:2]
    L = math.gcd(S, MLSTM_CHUNK)
    nc = S // L

    def to_chunks(t):
        return t.astype(jnp.float32).reshape((B, nc, L) + t.shape[2:]).swapaxes(0, 1)

    carry0 = (C.astype(jnp.float32), n.astype(jnp.float32), m.astype(jnp.float32))
    (C, n, m), h = lax.scan(mlstm_chunk, carry0, (to_chunks(q), to_chunks(k), to_chunks(v), to_chunks(ig), to_chunks(lf)))
    return h.swapaxes(0, 1).reshape(B, S, H_A, DV_A), C, n, m


def layer(x, c, pos, C0, n0, m0, conv0, past, p, l):
    B, S, _ = x.shape
    ada = (jax.nn.silu(c.astype(jnp.float32)) @ p['w_ada'].astype(jnp.float32) + p['b_ada'].astype(jnp.float32)).astype(x.dtype)
    sh1, sc1, gt1, sh2, sc2, gt2 = [a[:, None, :] for a in jnp.split(ada, N_ADA, axis=-1)]
    h = rmsnorm(x, p['g_norm1']) * (1 + sc1) + sh1
    proj = h @ p['w_in']
    qk_pre, v_a, o_a, i_a, f_a, q_b, k_b, v_b, g_a, g_b = jnp.split(proj, np.cumsum(SPLITS)[:-1].tolist(), axis=-1)

    xpad = jnp.concatenate([conv0.astype(x.dtype), qk_pre], axis=1)
    conv = p['conv_b']
    for j in range(CONV_W):
        conv = conv + p['conv_w'][j] * xpad[:, j:j + S]
    qk = jax.nn.silu(conv)
    q_a = qk[..., :W_QK_A].reshape(B, S, H_A, DQK_A)
    k_a = qk[..., W_QK_A:].reshape(B, S, H_A, DQK_A) * (DQK_A ** -0.5)
    ig = (i_a + p['b_if'][:H_A]).astype(jnp.float32)
    lf = jax.nn.log_sigmoid((f_a + p['b_if'][H_A:]).astype(jnp.float32))
    h_a, C1, n1, m1 = run_mlstm(q_a, k_a, v_a.reshape(B, S, H_A, DV_A), ig, lf, C0, n0, m0)
    y_a = rmsnorm(h_a.astype(x.dtype), p['g_mnorm']).reshape(B, S, W_V_A) * jax.nn.sigmoid(o_a)
    y_a = y_a @ p['w_branch_a']

    q_b = rope_partial(q_b.reshape(B, S, H_B, 2, D_QK_B), pos)
    k_b = rope_partial(k_b.reshape(B, S, H_B, 2, D_QK_B), pos)
    v_b = v_b.reshape(B, S, H_B, DV_B)
    lmb = p['lam'].astype(jnp.float32)
    lam_init = 0.8 - 0.6 * math.exp(-0.3 * l)
    lam = jnp.exp(jnp.sum(lmb[0] * lmb[1])) - jnp.exp(jnp.sum(lmb[2] * lmb[3])) + lam_init
    if past is None:
        nblk = S // Q_BLOCK
        qb = q_b.reshape(B, nblk, Q_BLOCK, H_B, 2, D_QK_B).swapaxes(0, 1)
        o = lax.map(lambda a: diff_attn_block(a[0], a[1], k_b, v_b, pos, lam), (qb, pos.reshape(nblk, Q_BLOCK)))
        o = o.swapaxes(0, 1).reshape(B, S, H_B, DV_B)
    else:
        pk, pv, ppos = past
        kk = jnp.concatenate([pk.astype(k_b.dtype), k_b], axis=1)
        vv = jnp.concatenate([pv.astype(v_b.dtype), v_b], axis=1)
        o = diff_attn_block(q_b, pos, kk, vv, jnp.concatenate([ppos, pos]), lam)
    y_b = (rmsnorm(o, p['g_dnorm']) * (1 - lam_init)).reshape(B, S, W_V_B) @ p['w_branch_b']

    mix = (jax.nn.sigmoid(g_a) * y_a + jax.nn.sigmoid(g_b) * y_b) @ p['w_out']
    x = x + gt1 * mix

    h2 = rmsnorm(x, p['g_norm2']) * (1 + sc2) + sh2
    g, u = jnp.split(h2 @ p['w_gate_up'], 2, axis=-1)
    x = x + gt2 * ((jax.nn.silu(g) * u) @ p['w_down'])
    return x, (k_b.reshape(B, S, H_B, 2 * D_QK_B), v_b, C1, n1, m1, xpad[:, -(CONV_W - 1):])


def setup_inputs(seed: int = 0) -> dict:
    key = jax.random.key(seed)
    ks = iter(jax.random.split(key, 40))
    f32 = jnp.float32
    nrm = lambda shape, scale: jax.random.normal(next(ks), shape, f32) * scale
    N_PAGES = PAST_LEN // PAGE_SIZE
    n_used = DEC_BATCH * N_PAGES
    N_POOL = n_used + max(1, n_used // 4)
    page_table = jax.random.permutation(next(ks), N_POOL)[:n_used].reshape(DEC_BATCH, N_PAGES).astype(jnp.int32)
    b_if = jnp.concatenate([nrm((DEPTH, H_A), 0.1),
                            jnp.broadcast_to(jnp.linspace(3.0, 6.0, H_A, dtype=f32), (DEPTH, H_A)) + nrm((DEPTH, H_A), 0.1)], axis=-1)
    return {
        'x_prompt': nrm((BATCH, SEQ, D_MODEL), 1.0),
        'x_sample': nrm((DEC_BATCH, DEC_SEQ, D_MODEL), 1.0),
        'c_prompt': nrm((BATCH, D_MODEL), 1.0),
        'c_sample': nrm((DEC_BATCH, D_MODEL), 1.0),
        'cache_k': nrm((DEPTH, N_POOL, PAGE_SIZE, H_B, 2 * D_QK_B), 1.0),
        'cache_v': nrm((DEPTH, N_POOL, PAGE_SIZE, H_B, DV_B), 1.0),
        'page_table': page_table,
        'state_C': nrm((DEPTH, DEC_BATCH, H_A, DV_A, DQK_A), 0.1),
        'state_n': nrm((DEPTH, DEC_BATCH, H_A, DQK_A), 0.1),
        'state_m': nrm((DEPTH, DEC_BATCH, H_A), 1.0),
        'state_conv': nrm((DEPTH, DEC_BATCH, CONV_W - 1, 2 * W_QK_A), 1.0),
        'g_norm1': 1.0 + nrm((DEPTH, D_MODEL), 0.02),
        'g_norm2': 1.0 + nrm((DEPTH, D_MODEL), 0.02),
        'w_ada': nrm((DEPTH, D_MODEL, N_ADA * D_MODEL), D_MODEL ** -0.5),
        'b_ada': nrm((DEPTH, N_ADA * D_MODEL), 0.02),
        'w_in': nrm((DEPTH, D_MODEL, N_IN), D_MODEL ** -0.5),
        'conv_w': nrm((DEPTH, CONV_W, 2 * W_QK_A), CONV_W ** -0.5),
        'conv_b': nrm((DEPTH, 2 * W_QK_A), 0.01),
        'b_if': b_if,
        'g_mnorm': 1.0 + nrm((DEPTH, H_A, DV_A), 0.02),
        'lam': nrm((DEPTH, 4, D_QK_B), 0.1),
        'g_dnorm': 1.0 + nrm((DEPTH, H_B, DV_B), 0.02),
        'w_branch_a': nrm((DEPTH, W_V_A, D_MODEL), W_V_A ** -0.5),
        'w_branch_b': nrm((DEPTH, W_V_B, D_MODEL), W_V_B ** -0.5),
        'w_out': nrm((DEPTH, D_MODEL, D_MODEL), D_MODEL ** -0.5),
        'w_gate_up': nrm((DEPTH, D_MODEL, 2 * D_FF), D_MODEL ** -0.5),
        'w_down': nrm((DEPTH, D_FF, D_MODEL), D_FF ** -0.5),
        'g_final': 1.0 + nrm((D_MODEL,), 0.02),
    }


def reference(x_prompt, x_sample, c_prompt, c_sample, cache_k, cache_v, page_table,
              state_C, state_n, state_m, state_conv,
              g_norm1, g_norm2, w_ada, b_ada, w_in, conv_w, conv_b, b_if, g_mnorm, lam, g_dnorm,
              w_branch_a, w_branch_b, w_out, w_gate_up, w_down, g_final):
    B, S, _ = x_prompt.shape
    DB, DS, _ = x_sample.shape
    past_len = page_table.shape[1] * cache_k.shape[2]
    pos_p = jnp.arange(S, dtype=jnp.int32)
    pos_s = past_len + jnp.arange(DS, dtype=jnp.int32)
    past_pos = jnp.arange(past_len, dtype=jnp.int32)
    xp, xs = x_prompt, x_sample
    outs_p, outs_s = [], []
    for l in range(DEPTH):
        p = {'g_norm1': g_norm1[l], 'g_norm2': g_norm2[l], 'w_ada': w_ada[l], 'b_ada': b_ada[l],
             'w_in': w_in[l], 'conv_w': conv_w[l], 'conv_b': conv_b[l], 'b_if': b_if[l],
             'g_mnorm': g_mnorm[l], 'lam': lam[l], 'g_dnorm': g_dnorm[l],
             'w_branch_a': w_branch_a[l], 'w_branch_b': w_branch_b[l], 'w_out': w_out[l],
             'w_gate_up': w_gate_up[l], 'w_down': w_down[l]}
        xp, st_p = layer(xp, c_prompt, pos_p,
                         jnp.zeros((B, H_A, DV_A, DQK_A), jnp.float32),
                         jnp.zeros((B, H_A, DQK_A), jnp.float32),
                         jnp.zeros((B, H_A), jnp.float32),
                         jnp.zeros((B, CONV_W - 1, 2 * W_QK_A), xp.dtype),
                         None, p, l)
        pk = cache_k[l][page_table].reshape(DB, past_len, H_B, 2, D_QK_B)
        pv = cache_v[l][page_table].reshape(DB, past_len, H_B, DV_B)
        xs, st_s = layer(xs, c_sample, pos_s, state_C[l], state_n[l], state_m[l], state_conv[l],
                         (pk, pv, past_pos), p, l)
        outs_p.append(st_p)
        outs_s.append(st_s)
    kp, vp, Cp, n_p, mp, convp = [jnp.stack(t) for t in zip(*outs_p)]
    k_s, v_s, C_s, n_s, m_s, conv_s = [jnp.stack(t) for t in zip(*outs_s)]
    y_prompt = rmsnorm(xp, g_final)
    y_sample = rmsnorm(xs, g_final)
    return (y_prompt, y_sample, kp, vp, Cp, n_p, mp, convp, k_s, v_s, C_s, n_s, m_s, conv_s)
```

```python
import functools
import math

import jax
import jax.numpy as jnp
from jax import lax
from jax.experimental import pallas as pl
from jax.experimental.pallas import tpu as pltpu

F32 = jnp.float32
BF16 = jnp.bfloat16

EPS = 1e-6
ROPE_THETA = 500000.0
NEG = -1e30
LANES = 128
SUBLANES = 8
VMEM_LIMIT = 48 * 1024 * 1024

_NT = (((1,), (1,)), ((), ()))
_TN = (((0,), (0,)), ((), ()))


def _cparams(*sem):
    return pltpu.CompilerParams(dimension_semantics=sem, vmem_limit_bytes=VMEM_LIMIT)


def _sigmoid(x):
    return 1.0 / (1.0 + jnp.exp(-x))


def _silu(x):
    return x * _sigmoid(x)


def _log_sigmoid(x):
    return jnp.minimum(x, 0.0) - jnp.log1p(jnp.exp(-jnp.abs(x)))


def _rms(x):
    return x * lax.rsqrt(jnp.mean(x * x, axis=-1, keepdims=True) + EPS)


def _ada_kernel(c_ref, w_ref, b_ref, o_ref):
    s = _silu(c_ref[...]).astype(BF16)
    o_ref[0] = jnp.dot(s, w_ref[0].astype(BF16), preferred_element_type=F32) + b_ref[0]


def _ada(c_all, w_ada, b_ada, tn=1024):
    depth, d, n = w_ada.shape
    rows = c_all.shape[0]
    return pl.pallas_call(
        _ada_kernel,
        grid=(depth, n // tn),
        in_specs=[
            pl.BlockSpec((rows, d), lambda l, j: (0, 0)),
            pl.BlockSpec((1, d, tn), lambda l, j: (l, 0, j)),
            pl.BlockSpec((1, 1, tn), lambda l, j: (l, 0, j)),
        ],
        out_specs=pl.BlockSpec((1, rows, tn), lambda l, j: (l, 0, j)),
        out_shape=jax.ShapeDtypeStruct((depth, rows, n), F32),
        compiler_params=_cparams("parallel", "parallel"),
        name="ada",
    )(c_all, w_ada, b_ada.reshape(depth, 1, n))


def _prenorm_kernel(x_ref, g_ref, sc_ref, sh_ref, o_ref):
    y = _rms(x_ref[...]) * g_ref[...]
    o_ref[...] = (y * (1.0 + sc_ref[0]) + sh_ref[0]).astype(o_ref.dtype)


def _mod_spec(mod, tm, tiles_per_group, ncol_block=None):
    _, r, d = mod.shape
    assert r in (1, tm)
    if ncol_block is None:
        return pl.BlockSpec((1, r, d), lambda i: (i // tiles_per_group, 0, 0))
    return pl.BlockSpec((1, r, ncol_block), lambda i, j: (i // tiles_per_group, 0, j))


def _prenorm(x, g, sc, sh, tm, tpg):
    m, d = x.shape
    return pl.pallas_call(
        _prenorm_kernel,
        grid=(m // tm,),
        in_specs=[
            pl.BlockSpec((tm, d), lambda i: (i, 0)),
            pl.BlockSpec((1, d), lambda i: (0, 0)),
            _mod_spec(sc, tm, tpg),
            _mod_spec(sh, tm, tpg),
        ],
        out_specs=pl.BlockSpec((tm, d), lambda i: (i, 0)),
        out_shape=jax.ShapeDtypeStruct((m, d), BF16),
        compiler_params=_cparams("parallel"),
        name="prenorm",
    )(x, g.reshape(1, d), sc, sh)


def _final_norm_kernel(x_ref, g_ref, o_ref):
    o_ref[...] = _rms(x_ref[...]) * g_ref[...]


def _final_norm(x, g, tm):
    m, d = x.shape
    return pl.pallas_call(
        _final_norm_kernel,
        grid=(m // tm,),
        in_specs=[pl.BlockSpec((tm, d), lambda i: (i, 0)), pl.BlockSpec((1, d), lambda i: (0, 0))],
        out_specs=pl.BlockSpec((tm, d), lambda i: (i, 0)),
        out_shape=jax.ShapeDtypeStruct((m, d), F32),
        compiler_params=_cparams("parallel"),
        name="final_norm",
    )(x, g.reshape(1, d))


def _proj_kernel(*refs, rope, half):
    a_ref, w_ref = refs[:2]
    outs = refs[5:] if rope else refs[2:]
    acc = jnp.dot(a_ref[...], w_ref[...], preferred_element_type=F32)
    if rope:
        ca_ref, cb_ref, cc_ref = refs[2:5]
        tn = acc.shape[1]
        reps = tn // LANES
        ca = jnp.tile(ca_ref[...], (1, reps))
        cb = jnp.tile(cb_ref[...], (1, reps))
        cc = jnp.tile(cc_ref[...], (1, reps))
        acc = acc * ca + pltpu.roll(acc, half, 1) * cb + pltpu.roll(acc, tn - half, 1) * cc
    for o in outs:
        o[...] = acc.astype(o.dtype)


def _proj(a, w, col0, ncols, out_dtypes, tm, tn, rope_tabs=None, half=None):
    m, k = a.shape
    assert col0 % tn == 0 and ncols % tn == 0 and m % tm == 0
    cb0 = col0 // tn
    in_specs = [
        pl.BlockSpec((tm, k), lambda i, j: (i, 0)),
        pl.BlockSpec((k, tn), lambda i, j: (0, cb0 + j)),
    ]
    args = [a, w]
    if rope_tabs is not None:
        for t in rope_tabs:
            in_specs.append(pl.BlockSpec((tm, LANES), lambda i, j: (i, 0)))
            args.append(t)
    outs = pl.pallas_call(
        functools.partial(_proj_kernel, rope=rope_tabs is not None, half=half),
        grid=(m // tm, ncols // tn),
        in_specs=in_specs,
        out_specs=[pl.BlockSpec((tm, tn), lambda i, j: (i, j)) for _ in out_dtypes],
        out_shape=[jax.ShapeDtypeStruct((m, ncols), dt) for dt in out_dtypes],
        compiler_params=_cparams("parallel", "parallel"),
        name="proj",
    )(*args)
    return outs


def _gates_kernel(a_ref, wc_ref, wr_ref, bc_ref, br_ref, oc_ref, or_ref, *, nh):
    a = a_ref[...]
    col = jnp.dot(a, wc_ref[...], preferred_element_type=F32) + bc_ref[...]
    row = lax.dot_general(wr_ref[...], a, _NT, preferred_element_type=F32) + br_ref[...]
    lane = lax.broadcasted_iota(jnp.int32, col.shape, 1)
    oc_ref[...] = jnp.where(lane < nh, col, _log_sigmoid(col))
    sub = lax.broadcasted_iota(jnp.int32, row.shape, 0)
    or_ref[...] = jnp.where(sub < nh, row, _log_sigmoid(row))


def _gates(a, w_if, b_if, nh, tm):
    m, k = a.shape
    ng = 2 * nh
    wc = jnp.zeros((k, LANES), BF16).at[:, :ng].set(w_if.astype(BF16))
    wr = jnp.zeros((2 * SUBLANES, k), BF16).at[:ng, :].set(w_if.T.astype(BF16))
    bc = jnp.zeros((1, LANES), F32).at[0, :ng].set(b_if)
    br = jnp.zeros((2 * SUBLANES, 1), F32).at[:ng, 0].set(b_if)
    return pl.pallas_call(
        functools.partial(_gates_kernel, nh=nh),
        grid=(m // tm,),
        in_specs=[
            pl.BlockSpec((tm, k), lambda i: (i, 0)),
            pl.BlockSpec((k, LANES), lambda i: (0, 0)),
            pl.BlockSpec((2 * SUBLANES, k), lambda i: (0, 0)),
            pl.BlockSpec((1, LANES), lambda i: (0, 0)),
            pl.BlockSpec((2 * SUBLANES, 1), lambda i: (0, 0)),
        ],
        out_specs=[
            pl.BlockSpec((tm, LANES), lambda i: (i, 0)),
            pl.BlockSpec((2 * SUBLANES, tm), lambda i: (0, i)),
        ],
        out_shape=[
            jax.ShapeDtypeStruct((m, LANES), F32),
            jax.ShapeDtypeStruct((2 * SUBLANES, m), F32),
        ],
        compiler_params=_cparams("parallel"),
        name="gates",
    )(a, wc, wr, bc, br)


def _conv_kernel(x_ref, prev_ref, c0_ref, w_ref, b_ref, sc_ref, o_ref, buf, *, ts, cw):
    first = pl.program_id(1) == 0
    buf[0:SUBLANES, :] = jnp.where(first, c0_ref[0], prev_ref[0])
    buf[SUBLANES:SUBLANES + ts, :] = x_ref[0]
    acc = b_ref[...]
    for j in range(cw):
        lo = SUBLANES - (cw - 1) + j
        acc = acc + w_ref[j:j + 1, :] * buf[lo:lo + ts, :]
    o_ref[0] = (_silu(acc) * sc_ref[...]).astype(o_ref.dtype)


def _conv_silu(x, conv0, conv_w, conv_b, out_scale, ts):
    b, s, c = x.shape
    cw = conv_w.shape[0]
    c0 = jnp.zeros((b, SUBLANES, c), F32).at[:, SUBLANES - (cw - 1):, :].set(conv0)
    wpad = jnp.zeros((SUBLANES, c), F32).at[:cw].set(conv_w)
    rb = ts // SUBLANES
    return pl.pallas_call(
        functools.partial(_conv_kernel, ts=ts, cw=cw),
        grid=(b, s // ts),
        in_specs=[
            pl.BlockSpec((1, ts, c), lambda bi, i: (bi, i, 0)),
            pl.BlockSpec((1, SUBLANES, c), lambda bi, i: (bi, jnp.maximum(i * rb - 1, 0), 0)),
            pl.BlockSpec((1, SUBLANES, c), lambda bi, i: (bi, 0, 0)),
            pl.BlockSpec((SUBLANES, c), lambda bi, i: (0, 0)),
            pl.BlockSpec((1, c), lambda bi, i: (0, 0)),
            pl.BlockSpec((1, c), lambda bi, i: (0, 0)),
        ],
        out_specs=pl.BlockSpec((1, ts, c), lambda bi, i: (bi, i, 0)),
        out_shape=jax.ShapeDtypeStruct((b, s, c), BF16),
        scratch_shapes=[pltpu.VMEM((ts + SUBLANES, c), F32)],
        compiler_params=_cparams("parallel", "parallel"),
        name="conv_silu",
    )(x, x, c0, wpad, conv_b.reshape(1, c), out_scale.reshape(1, c))


def _mlstm_kernel(qk_ref, v_ref, gc_ref, gr_ref, oa_ref, c0_ref, n0_ref, m0_ref, gm_ref,
                  y_ref, c1_ref, n1_ref, m1_ref, cs, ms, *, nh, dk, dv, chunk):
    c = pl.program_id(1)
    nc = pl.num_programs(1)
    ext = dv + LANES
    L = chunk

    @pl.when(c == 0)
    def _():
        for h in range(nh):
            cs[h, 0:dv, :] = c0_ref[0, h]
            cs[h, dv:dv + SUBLANES, :] = n0_ref[0, h]
            cs[h, dv + SUBLANES:ext, :] = jnp.zeros((LANES - SUBLANES, dk), F32)
        ms[...] = m0_ref[0]

    row_t = lax.broadcasted_iota(jnp.int32, (L, L), 0)
    col_s = lax.broadcasted_iota(jnp.int32, (L, L), 1)
    causal = col_s <= row_t
    upper = col_s >= row_t
    gr = gr_ref[0]
    gc = gc_ref[0]
    one_col = (lax.broadcasted_iota(jnp.int32, (L, LANES), 1) == 0).astype(BF16)

    for h in range(nh):
        q = qk_ref[0, :, h * dk:(h + 1) * dk]
        k = qk_ref[0, :, nh * dk + h * dk:nh * dk + (h + 1) * dk]
        v = v_ref[0, :, h * dv:(h + 1) * dv]
        ig_row = gr[h:h + 1, :]
        lf_row = gr[nh + h:nh + h + 1, :]
        ig_col = gc[:, h:h + 1]
        lf_col = gc[:, nh + h:nh + h + 1]
        m_prev = ms[...][h:h + 1, 0:1]

        b_col = jnp.sum(jnp.where(causal, lf_row, 0.0), axis=1, keepdims=True)
        b_row = jnp.sum(jnp.where(upper, lf_col, 0.0), axis=0, keepdims=True)
        a_row = ig_row - b_row
        a_col = ig_col - b_col
        g_col = jnp.maximum(jnp.max(jnp.where(causal, a_row, NEG), axis=1, keepdims=True), m_prev)
        d = jnp.exp(jnp.where(causal, a_row - g_col, NEG))
        s = lax.dot_general(q, k, _NT, preferred_element_type=F32) * d
        v_ext = jnp.concatenate([v, one_col], axis=1)
        state = cs[h]
        r1 = jnp.dot(s.astype(BF16), v_ext, preferred_element_type=F32)
        r2 = lax.dot_general(q, state.astype(BF16), _NT, preferred_element_type=F32)
        tot = r1 + jnp.exp(m_prev - g_col) * r2
        num = tot[:, 0:dv]
        den = tot[:, dv:dv + 1]
        hh = num / jnp.maximum(jnp.abs(den), jnp.exp(-(b_col + g_col)))
        y = _rms(hh) * gm_ref[:, h * dv:(h + 1) * dv] * _sigmoid(oa_ref[0, :, h * dv:(h + 1) * dv])
        y_ref[0, :, h * dv:(h + 1) * dv] = y.astype(y_ref.dtype)

        g_last = g_col[L - 1:L, :]
        vw = (v_ext.astype(F32) * jnp.exp(a_col - g_last)).astype(BF16)
        upd = lax.dot_general(vw, k, _TN, preferred_element_type=F32)
        cs[h] = jnp.exp(m_prev - g_last) * state + upd
        ms[h:h + 1, :] = jnp.broadcast_to(b_col[L - 1:L, :] + g_last, (1, LANES))

    @pl.when(c == nc - 1)
    def _():
        for h in range(nh):
            c1_ref[0, h] = cs[h, 0:dv, :]
            n1_ref[0, h] = cs[h, dv:dv + SUBLANES, :]
        m1_ref[0] = ms[...]


def _mlstm(qk, v, gcol, grow, oa, c0, n0, m0, g_mnorm, chunk):
    b, s, _ = qk.shape
    _, nh, dv, dk = c0.shape
    nc = s // chunk
    n0p = jnp.zeros((b, nh, SUBLANES, dk), F32).at[:, :, 0, :].set(n0)
    m0p = jnp.zeros((b, SUBLANES, LANES), F32).at[:, :nh, :].set(
        jnp.broadcast_to(m0[:, :, None], (b, nh, LANES)))
    y, c1, n1, m1 = pl.pallas_call(
        functools.partial(_mlstm_kernel, nh=nh, dk=dk, dv=dv, chunk=chunk),
        grid=(b, nc),
        in_specs=[
            pl.BlockSpec((1, chunk, 2 * nh * dk), lambda bi, ci: (bi, ci, 0)),
            pl.BlockSpec((1, chunk, nh * dv), lambda bi, ci: (bi, ci, 0)),
            pl.BlockSpec((1, chunk, LANES), lambda bi, ci: (bi, ci, 0)),
            pl.BlockSpec((1, 2 * SUBLANES, chunk), lambda bi, ci: (bi, 0, ci)),
            pl.BlockSpec((1, chunk, nh * dv), lambda bi, ci: (bi, ci, 0)),
            pl.BlockSpec((1, nh, dv, dk), lambda bi, ci: (bi, 0, 0, 0)),
            pl.BlockSpec((1, nh, SUBLANES, dk), lambda bi, ci: (bi, 0, 0, 0)),
            pl.BlockSpec((1, SUBLANES, LANES), lambda bi, ci: (bi, 0, 0)),
            pl.BlockSpec((1, nh * dv), lambda bi, ci: (0, 0)),
        ],
        out_specs=[
            pl.BlockSpec((1, chunk, nh * dv), lambda bi, ci: (bi, ci, 0)),
            pl.BlockSpec((1, nh, dv, dk), lambda bi, ci: (bi, 0, 0, 0)),
            pl.BlockSpec((1, nh, SUBLANES, dk), lambda bi, ci: (bi, 0, 0, 0)),
            pl.BlockSpec((1, SUBLANES, LANES), lambda bi, ci: (bi, 0, 0)),
        ],
        out_shape=[
            jax.ShapeDtypeStruct((b, s, nh * dv), BF16),
            jax.ShapeDtypeStruct((b, nh, dv, dk), F32),
            jax.ShapeDtypeStruct((b, nh, SUBLANES, dk), F32),
            jax.ShapeDtypeStruct((b, SUBLANES, LANES), F32),
        ],
        scratch_shapes=[pltpu.VMEM((nh, dv + LANES, dk), F32), pltpu.VMEM((SUBLANES, LANES), F32)],
        compiler_params=_cparams("parallel", "arbitrary"),
        name="mlstm",
    )(qk, v, gcol, grow, oa, c0, n0p, m0p, g_mnorm.reshape(1, nh * dv))
    return y, c1, n1[:, :, 0, :], m1[:, :nh, 0]


def _lam_value(lam_ref, lam_init):
    p = lam_ref[...]
    s01 = jnp.sum(p[0:1] * p[1:2], axis=1, keepdims=True)
    s23 = jnp.sum(p[2:3] * p[3:4], axis=1, keepdims=True)
    return jnp.exp(s01) - jnp.exp(s23) + lam_init


def _flash_kernel(lam_ref, q_ref, k_ref, v_ref, gd_ref, o_ref, m_s, l_s, acc_s,
                  *, tq, tk, dq, scale, lam_init):
    qi = pl.program_id(2)
    ki = pl.program_id(3)
    nk = pl.num_programs(3)

    @pl.when(ki == 0)
    def _():
        m_s[...] = jnp.full(m_s.shape, NEG, F32)
        l_s[...] = jnp.zeros(l_s.shape, F32)
        acc_s[...] = jnp.zeros(acc_s.shape, F32)

    @pl.when(ki * tk <= qi * tq + (tq - 1))
    def _():
        q = q_ref[0]
        k = k_ref[0]
        v = v_ref[0]
        rows = qi * tq + lax.broadcasted_iota(jnp.int32, (tq, tk), 0)
        cols = ki * tk + lax.broadcasted_iota(jnp.int32, (tq, tk), 1)
        keep = cols <= rows
        for mi in range(2):
            s = lax.dot_general(q[:, mi * dq:(mi + 1) * dq], k[:, mi * dq:(mi + 1) * dq], _NT,
                                preferred_element_type=F32) * scale
            s = jnp.where(keep, s, NEG)
            m_prev = m_s[mi]
            m_new = jnp.maximum(m_prev, jnp.max(s, axis=1, keepdims=True))
            alpha = jnp.exp(m_prev - m_new)
            p = jnp.exp(s - m_new)
            l_s[mi] = alpha * l_s[mi] + jnp.sum(p, axis=1, keepdims=True)
            acc_s[mi] = alpha * acc_s[mi] + jnp.dot(p.astype(BF16), v, preferred_element_type=F32)
            m_s[mi] = m_new

    @pl.when(ki == nk - 1)
    def _():
        lam = _lam_value(lam_ref, lam_init)
        o = acc_s[0] / l_s[0] - lam * (acc_s[1] / l_s[1])
        o_ref[0] = (_rms(o) * gd_ref[0] * (1.0 - lam_init)).astype(o_ref.dtype)


def _flash(q, k, v, lam_p, g_dnorm, lam_init, tq, tk):
    b, s, w = q.shape
    nh, dv = g_dnorm.shape
    dq = dv // 2

    def kv_map(bi, h, qi, ki):
        return (bi, jnp.minimum(ki, (qi * tq + tq - 1) // tk), h)

    return pl.pallas_call(
        functools.partial(_flash_kernel, tq=tq, tk=tk, dq=dq, scale=dq ** -0.5, lam_init=lam_init),
        grid=(b, nh, s // tq, s // tk),
        in_specs=[
            pl.BlockSpec(lam_p.shape, lambda bi, h, qi, ki: (0, 0)),
            pl.BlockSpec((1, tq, dv), lambda bi, h, qi, ki: (bi, qi, h)),
            pl.BlockSpec((1, tk, dv), kv_map),
            pl.BlockSpec((1, tk, dv), kv_map),
            pl.BlockSpec((1, 1, dv), lambda bi, h, qi, ki: (h, 0, 0)),
        ],
        out_specs=pl.BlockSpec((1, tq, dv), lambda bi, h, qi, ki: (bi, qi, h)),
        out_shape=jax.ShapeDtypeStruct((b, s, w), BF16),
        scratch_shapes=[
            pltpu.VMEM((2, tq, 1), F32),
            pltpu.VMEM((2, tq, 1), F32),
            pltpu.VMEM((2, tq, dv), F32),
        ],
        compiler_params=_cparams("parallel", "parallel", "parallel", "arbitrary"),
        name="flash_diff",
    )(lam_p, q, k, v, g_dnorm.reshape(nh, 1, dv))


def _decode_kernel(pt_ref, lam_ref, q_ref, bias_ref, *rest, npg, nh, ds, dv, scale, lam_init):
    k_refs = rest[:npg]
    v_refs = rest[npg:2 * npg]
    kn_ref, vn_ref, biasn_ref, gd_ref, o_ref, m_s, l_s, acc_s = rest[2 * npg:]
    j = pl.program_id(1)
    nj = pl.num_programs(1)

    @pl.when(j == 0)
    def _():
        m_s[...] = jnp.full(m_s.shape, NEG, F32)
        l_s[...] = jnp.zeros(l_s.shape, F32)
        acc_s[...] = jnp.zeros(acc_s.shape, F32)

    q = q_ref[0]

    def update(kk, vv, bias):
        s = lax.dot_general(q, kk, _NT, preferred_element_type=F32) * scale + bias
        m_prev = m_s[...]
        m_new = jnp.maximum(m_prev, jnp.max(s, axis=1, keepdims=True))
        alpha = jnp.exp(m_prev - m_new)
        p = jnp.exp(s - m_new)
        l_s[...] = alpha * l_s[...] + jnp.sum(p, axis=1, keepdims=True)
        acc_s[...] = alpha * acc_s[...] + jnp.dot(p.astype(BF16), vv, preferred_element_type=F32)
        m_s[...] = m_new

    for i in range(npg):
        kp = k_refs[i][...]
        vp = v_refs[i][...]
        rows = kp.shape[0] * kp.shape[1]
        update(kp.reshape(rows, dv).astype(BF16), vp.reshape(rows, dv).astype(BF16), bias_ref[...])

    @pl.when(j == nj - 1)
    def _():
        update(kn_ref[0], vn_ref[0], biasn_ref[...])
        lam = _lam_value(lam_ref, lam_init)
        o_all = (acc_s[...] / l_s[...]).reshape(nh, 2, ds, dv)
        o = o_all[:, 0] - lam * o_all[:, 1]
        o_ref[0] = (_rms(o) * gd_ref[...] * (1.0 - lam_init)).astype(o_ref.dtype)


def _decode_attn(q, k_new, v_new, cache_k, cache_v, layer, page_table, lam_p, g_dnorm, lam_init, npg=4):
    db, ds, w = q.shape
    nh, dv = g_dnorm.shape
    dq = dv // 2
    page = cache_k.shape[2]
    n_pages = page_table.shape[1]
    assert n_pages % npg == 0
    nrow = nh * 2 * ds

    q5 = q.reshape(db, ds, nh, 2, dq).transpose(0, 2, 3, 1, 4)
    zero = jnp.zeros_like(q5[:, :, 0])
    qbd = jnp.stack([jnp.concatenate([q5[:, :, 0], zero], -1),
                     jnp.concatenate([zero, q5[:, :, 1]], -1)], axis=2).reshape(db, nrow, dv)

    r_head = (jnp.arange(nrow) // (2 * ds))[:, None]
    r_tok = (jnp.arange(nrow) % ds)[:, None]
    c_head = (jnp.arange(page * nh) % nh)[None, :]
    bias = jnp.where(c_head == r_head, 0.0, NEG).astype(F32)
    cn_head = (jnp.arange(ds * nh) % nh)[None, :]
    cn_tok = (jnp.arange(ds * nh) // nh)[None, :]
    bias_new = jnp.where((cn_head == r_head) & (cn_tok <= r_tok), 0.0, NEG).astype(F32)

    kn = k_new.reshape(db, ds * nh, dv)
    vn = v_new.reshape(db, ds * nh, dv)
    pt = page_table.reshape(-1).astype(jnp.int32)

    def page_map(i):
        return lambda bi, j, pt_ref: (layer, pt_ref[bi * n_pages + j * npg + i], 0, 0, 0)

    page_spec = [pl.BlockSpec((None, None, page, nh, dv), page_map(i)) for i in range(npg)]
    grid_spec = pltpu.PrefetchScalarGridSpec(
        num_scalar_prefetch=1,
        grid=(db, n_pages // npg),
        in_specs=[
            pl.BlockSpec(lam_p.shape, lambda bi, j, pt_ref: (0, 0)),
            pl.BlockSpec((1, nrow, dv), lambda bi, j, pt_ref: (bi, 0, 0)),
            pl.BlockSpec(bias.shape, lambda bi, j, pt_ref: (0, 0)),
            *page_spec, *page_spec,
            pl.BlockSpec((1, ds * nh, dv), lambda bi, j, pt_ref: (bi, 0, 0)),
            pl.BlockSpec((1, ds * nh, dv), lambda bi, j, pt_ref: (bi, 0, 0)),
            pl.BlockSpec(bias_new.shape, lambda bi, j, pt_ref: (0, 0)),
            pl.BlockSpec((nh, 1, dv), lambda bi, j, pt_ref: (0, 0, 0)),
        ],
        out_specs=pl.BlockSpec((1, nh, ds, dv), lambda bi, j, pt_ref: (bi, 0, 0, 0)),
        scratch_shapes=[
            pltpu.VMEM((nrow, 1), F32),
            pltpu.VMEM((nrow, 1), F32),
            pltpu.VMEM((nrow, dv), F32),
        ],
    )
    o = pl.pallas_call(
        functools.partial(_decode_kernel, npg=npg, nh=nh, ds=ds, dv=dv, scale=dq ** -0.5,
                          lam_init=lam_init),
        grid_spec=grid_spec,
        out_shape=jax.ShapeDtypeStruct((db, nh, ds, dv), BF16),
        compiler_params=_cparams("parallel", "arbitrary"),
        name="decode_attn",
    )(pt, lam_p, qbd, bias, *([cache_k] * npg), *([cache_v] * npg), kn, vn, bias_new,
      g_dnorm.reshape(nh, 1, dv))
    return o.transpose(0, 2, 1, 3).reshape(db, ds, w)


def _merge_kernel(a_ref, b_ref, wa_ref, wb_ref, ga_ref, gb_ref, o_ref):
    ya = jnp.dot(a_ref[...], wa_ref[...], preferred_element_type=F32)
    yb = jnp.dot(b_ref[...], wb_ref[...], preferred_element_type=F32)
    o_ref[...] = (_sigmoid(ga_ref[...]) * ya + _sigmoid(gb_ref[...]) * yb).astype(o_ref.dtype)


def _merge(a, b, wa, wb, ga, gb, tm, tn):
    m, k = a.shape
    n = wa.shape[1]
    return pl.pallas_call(
        _merge_kernel,
        grid=(m // tm, n // tn),
        in_specs=[
            pl.BlockSpec((tm, k), lambda i, j: (i, 0)),
            pl.BlockSpec((tm, k), lambda i, j: (i, 0)),
            pl.BlockSpec((k, tn), lambda i, j: (0, j)),
            pl.BlockSpec((k, tn), lambda i, j: (0, j)),
            pl.BlockSpec((tm, tn), lambda i, j: (i, j)),
            pl.BlockSpec((tm, tn), lambda i, j: (i, j)),
        ],
        out_specs=pl.BlockSpec((tm, tn), lambda i, j: (i, j)),
        out_shape=jax.ShapeDtypeStruct((m, n), BF16),
        compiler_params=_cparams("parallel", "parallel"),
        name="merge",
    )(a, b, wa, wb, ga, gb)


def _resid_kernel(a_ref, w_ref, x_ref, gt_ref, o_ref):
    y = jnp.dot(a_ref[...], w_ref[...], preferred_element_type=F32)
    o_ref[...] = x_ref[...] + gt_ref[0] * y


def _resid_matmul(a, w, x, gt, tm, tn, tpg):
    m, k = a.shape
    n = w.shape[1]
    r = gt.shape[1]
    return pl.pallas_call(
        _resid_kernel,
        grid=(m // tm, n // tn),
        in_specs=[
            pl.BlockSpec((tm, k), lambda i, j: (i, 0)),
            pl.BlockSpec((k, tn), lambda i, j: (0, j)),
            pl.BlockSpec((tm, tn), lambda i, j: (i, j)),
            pl.BlockSpec((1, r, tn), lambda i, j: (i // tpg, 0, j)),
        ],
        out_specs=pl.BlockSpec((tm, tn), lambda i, j: (i, j)),
        out_shape=jax.ShapeDtypeStruct((m, n), F32),
        compiler_params=_cparams("parallel", "parallel"),
        name="resid_matmul",
    )(a, w, x, gt)


def _swiglu_kernel(a_ref, wg_ref, wu_ref, o_ref):
    a = a_ref[...]
    g = jnp.dot(a, wg_ref[...], preferred_element_type=F32)
    u = jnp.dot(a, wu_ref[...], preferred_element_type=F32)
    o_ref[...] = (_silu(g) * u).astype(o_ref.dtype)


def _swiglu(a, w_gu, tm, tn):
    m, k = a.shape
    dff = w_gu.shape[1] // 2
    nb = dff // tn
    return pl.pallas_call(
        _swiglu_kernel,
        grid=(m // tm, nb),
        in_specs=[
            pl.BlockSpec((tm, k), lambda i, j: (i, 0)),
            pl.BlockSpec((k, tn), lambda i, j: (0, j)),
            pl.BlockSpec((k, tn), lambda i, j: (0, nb + j)),
        ],
        out_specs=pl.BlockSpec((tm, tn), lambda i, j: (i, j)),
        out_shape=jax.ShapeDtypeStruct((m, dff), BF16),
        compiler_params=_cparams("parallel", "parallel"),
        name="swiglu",
    )(a, w_gu, w_gu)


def _rope_tables(pos, dq, reps):
    rot = dq // 4
    half = rot // 2
    inv = jnp.power(ROPE_THETA, -jnp.arange(0, rot, 2, dtype=F32) / rot)
    ang = pos.astype(F32)[:, None] * inv[None, :]
    cos, sin = jnp.cos(ang), jnp.sin(ang)
    n = pos.shape[0]
    one = jnp.ones((n, dq - rot), F32)
    zero_h = jnp.zeros((n, half), F32)
    zero_r = jnp.zeros((n, dq - rot), F32)
    ta = jnp.concatenate([cos, cos, one], axis=1)
    tb = jnp.concatenate([zero_h, sin, zero_r], axis=1)
    tc = jnp.concatenate([-sin, zero_h, zero_r], axis=1)
    return [jnp.tile(t, (reps, 1)) for t in (ta, tb, tc)], half


def _layer(x, mods, wts, st, geom, layer_idx, past):
    b, s, tm, tpg, chunk = geom
    m, d = x.shape
    sh1, sc1, gt1, sh2, sc2, gt2 = mods
    nh_a, dv_a, dk_a = st["C"].shape[1:]
    nh_b, dv_b = wts["g_dnorm"].shape
    dq_b = dv_b // 2
    w_qk_a, w_v_a = nh_a * dk_a, nh_a * dv_a
    w_b = nh_b * dv_b
    lam_init = 0.8 - 0.6 * math.exp(-0.3 * layer_idx)
    tn = 512

    h1 = _prenorm(x, wts["g_norm1"], sc1, sh1, tm, tpg)

    wa, wb = wts["w_in_a"], wts["w_in_b"]
    (qk_pre,) = _proj(h1, wa, 0, 2 * w_qk_a, (F32,), tm, tn)
    (v_a,) = _proj(h1, wa, 2 * w_qk_a, w_v_a, (BF16,), tm, tn)
    (o_a,) = _proj(h1, wa, 2 * w_qk_a + w_v_a, w_v_a, (F32,), tm, tn)
    gcol, grow = _gates(h1, wts["w_if"], wts["b_if"], nh_a, tm)

    pos = st["pos"]
    tabs, half = _rope_tables(pos, dq_b, b)
    (q_b,) = _proj(h1, wb, 0, w_b, (BF16,), tm, tn, tabs, half)
    k_f, k_h = _proj(h1, wb, w_b, w_b, (F32, BF16), tm, tn, tabs, half)
    v_f, v_h = _proj(h1, wb, 2 * w_b, w_b, (F32, BF16), tm, tn)
    (g_a,) = _proj(h1, wb, 3 * w_b, d, (F32,), tm, tn)
    (g_b,) = _proj(h1, wb, 3 * w_b + d, d, (F32,), tm, tn)

    qk_pre3 = qk_pre.reshape(b, s, 2 * w_qk_a)
    k_scale = jnp.concatenate([jnp.ones((w_qk_a,), F32), jnp.full((w_qk_a,), dk_a ** -0.5, F32)])
    qk_a = _conv_silu(qk_pre3, st["conv"], wts["conv_w"], wts["conv_b"], k_scale, min(s, 256))
    grow3 = grow.reshape(2 * SUBLANES, b, s).transpose(1, 0, 2)
    y_a, c1, n1, m1 = _mlstm(qk_a, v_a.reshape(b, s, w_v_a), gcol.reshape(b, s, LANES), grow3,
                             o_a.reshape(b, s, w_v_a), st["C"], st["n"], st["m"],
                             wts["g_mnorm"], chunk)
    cw = wts["conv_w"].shape[0]
    assert s >= cw - 1
    conv1 = qk_pre3[:, s - (cw - 1):]

    if past is None:
        y_b = _flash(q_b.reshape(b, s, w_b), k_h.reshape(b, s, w_b), v_h.reshape(b, s, w_b),
                     wts["lam"], wts["g_dnorm"], lam_init, 1024, 512)
    else:
        cache_k, cache_v, page_table = past
        y_b = _decode_attn(q_b.reshape(b, s, w_b), k_h.reshape(b, s, nh_b, dv_b),
                           v_h.reshape(b, s, nh_b, dv_b), cache_k, cache_v, layer_idx, page_table,
                           wts["lam"], wts["g_dnorm"], lam_init)

    mix = _merge(y_a.reshape(m, w_v_a), y_b.reshape(m, w_b), wts["w_branch_a"], wts["w_branch_b"],
                 g_a, g_b, min(tm, 512), tn)
    x = _resid_matmul(mix, wts["w_out"], x, gt1, tm, tn, tpg)

    h2 = _prenorm(x, wts["g_norm2"], sc2, sh2, tm, tpg)
    hid = _swiglu(h2, wts["w_gate_up"], tm, tn)
    tmd = min(tm, 512)
    x = _resid_matmul(hid, wts["w_down"], x, gt2, tmd, tn, tpg * (tm // tmd))

    outs = (k_f.reshape(b, s, nh_b, dv_b), v_f.reshape(b, s, nh_b, dv_b), c1, n1, m1, conv1)
    return x, outs


def kernel(x_prompt, x_sample, c_prompt, c_sample, cache_k, cache_v, page_table, state_C, state_n,
           state_m, state_conv, g_norm1, g_norm2, w_ada, b_ada, w_in, conv_w, conv_b, b_if, g_mnorm,
           lam, g_dnorm, w_branch_a, w_branch_b, w_out, w_gate_up, w_down, g_final):
    bp, sp, d = x_prompt.shape
    db, ds, _ = x_sample.shape
    depth = w_in.shape[0]
    nh_a, dv_a, dk_a = state_C.shape[2:]
    n_ada = w_ada.shape[2] // d
    past_len = page_table.shape[1] * cache_k.shape[2]
    w_a_cols = 2 * nh_a * dk_a + 2 * nh_a * dv_a
    n_if = 2 * nh_a

    rows = 2 * SUBLANES
    assert bp + db <= rows
    c_all = jnp.zeros((rows, d), F32).at[:bp].set(c_prompt).at[bp:bp + db].set(c_sample)
    ada = _ada(c_all, w_ada, b_ada)

    xp = x_prompt.reshape(bp * sp, d)
    xs = x_sample.reshape(db * ds, d)
    tm_p = 1024
    geom_p = (bp, sp, tm_p, sp // tm_p, 256)
    geom_s = (db, ds, db * ds, 1, ds)
    pos_p = jnp.arange(sp, dtype=jnp.int32)
    pos_s = past_len + jnp.arange(ds, dtype=jnp.int32)

    outs_p, outs_s = [], []
    for l in range(depth):
        wts = {
            "g_norm1": g_norm1[l], "g_norm2": g_norm2[l],
            "w_in_a": w_in[l][:, :w_a_cols].astype(BF16),
            "w_if": w_in[l][:, w_a_cols:w_a_cols + n_if],
            "w_in_b": w_in[l][:, w_a_cols + n_if:].astype(BF16),
            "b_if": b_if[l], "conv_w": conv_w[l], "conv_b": conv_b[l], "g_mnorm": g_mnorm[l],
            "lam": lam[l], "g_dnorm": g_dnorm[l],
            "w_branch_a": w_branch_a[l].astype(BF16), "w_branch_b": w_branch_b[l].astype(BF16),
            "w_out": w_out[l].astype(BF16), "w_gate_up": w_gate_up[l].astype(BF16),
            "w_down": w_down[l].astype(BF16),
        }
        a_l = ada[l].reshape(rows, n_ada, d)
        mods_p = [a_l[:bp, i][:, None, :] for i in range(n_ada)]
        mods_s = [jnp.repeat(a_l[bp:bp + db, i], ds, axis=0)[None] for i in range(n_ada)]

        st_p = {
            "C": jnp.zeros((bp, nh_a, dv_a, dk_a), F32), "n": jnp.zeros((bp, nh_a, dk_a), F32),
            "m": jnp.zeros((bp, nh_a), F32), "conv": jnp.zeros((bp,) + state_conv.shape[2:], F32),
            "pos": pos_p,
        }
        xp, o_p = _layer(xp, mods_p, wts, st_p, geom_p, l, None)
        st_s = {"C": state_C[l], "n": state_n[l], "m": state_m[l], "conv": state_conv[l], "pos": pos_s}
        xs, o_s = _layer(xs, mods_s, wts, st_s, geom_s, l, (cache_k, cache_v, page_table))
        outs_p.append(o_p)
        outs_s.append(o_s)

    kp, vp, cp, n_p, mp, convp = [jnp.stack(t) for t in zip(*outs_p)]
    k_s, v_s, c_s, n_s, m_s, conv_s = [jnp.stack(t) for t in zip(*outs_s)]
    y_prompt = _final_norm(xp, g_final, tm_p).reshape(bp, sp, d)
    y_sample = _final_norm(xs, g_final, db * ds).reshape(db, ds, d)
    return (y_prompt, y_sample, kp, vp, cp, n_p, mp, convp, k_s, v_s, c_s, n_s, m_s, conv_s)
```

```python
import functools
import math

import jax
import jax.numpy as jnp
import numpy as np
from jax import lax
from jax.experimental import pallas as pl
from jax.experimental.pallas import tpu as pltpu

F32 = jnp.float32
BF16 = jnp.bfloat16

EPS = 1e-6
ROPE_THETA = 500000.0
NEG = -1e30
LOG2E = math.log2(math.e)
LANES = 128
SUBLANES = 8
VMEM_LIMIT = 56 * 1024 * 1024

TM_ELEMWISE = 512
TM_MATMUL = 2048
TM_TWO_OPERAND = 1024
TN = 512
TN_SWIGLU = 256
TN_DOWN = 256
ATT_T = 512
ATT_NSUB = 4
MLSTM_CHUNK = 256

_NT = (((1,), (1,)), ((), ()))
_TN = (((0,), (0,)), ((), ()))


def _cparams(*sem):
    return pltpu.CompilerParams(dimension_semantics=sem, vmem_limit_bytes=VMEM_LIMIT)


def _sigmoid(x):
    return 1.0 / (1.0 + jnp.exp(-x))


def _silu(x):
    return x * _sigmoid(x)


def _log_sigmoid(x):
    return jnp.minimum(x, 0.0) - jnp.log1p(jnp.exp(-jnp.abs(x)))


def _rms(x, axis=-1):
    return x * lax.rsqrt(jnp.mean(x * x, axis=axis, keepdims=True) + EPS)


def _ada_kernel(c_ref, w_ref, b_ref, o_ref):
    s = _silu(c_ref[...]).astype(BF16)
    o_ref[0] = jnp.dot(s, w_ref[0].astype(BF16), preferred_element_type=F32) + b_ref[0]


def _ada(c_all, w_ada, b_ada, tn=1024):
    depth, d, n = w_ada.shape
    rows = c_all.shape[0]
    return pl.pallas_call(
        _ada_kernel,
        grid=(depth, n // tn),
        in_specs=[
            pl.BlockSpec((rows, d), lambda l, j: (0, 0)),
            pl.BlockSpec((1, d, tn), lambda l, j: (l, 0, j)),
            pl.BlockSpec((1, 1, tn), lambda l, j: (l, 0, j)),
        ],
        out_specs=pl.BlockSpec((1, rows, tn), lambda l, j: (l, 0, j)),
        out_shape=jax.ShapeDtypeStruct((depth, rows, n), F32),
        compiler_params=_cparams("parallel", "parallel"),
        name="ada",
    )(c_all, w_ada, b_ada.reshape(depth, 1, n))


def _prenorm_kernel(x_ref, g_ref, sc_ref, sh_ref, o_ref):
    y = _rms(x_ref[...]) * g_ref[...]
    o_ref[...] = (y * (1.0 + sc_ref[0]) + sh_ref[0]).astype(o_ref.dtype)


def _row_tile(m, rpg, pref):
    tm = min(pref, m, rpg)
    assert m % tm == 0 and rpg % tm == 0
    return tm, rpg // tm


def _prenorm(x, g, sc, sh, rpg):
    m, d = x.shape
    tm, tpg = _row_tile(m, rpg, TM_ELEMWISE)
    r = sc.shape[1]
    assert r in (1, tm)
    mod = pl.BlockSpec((1, r, d), lambda i: (i // tpg, 0, 0))
    return pl.pallas_call(
        _prenorm_kernel,
        grid=(m // tm,),
        in_specs=[
            pl.BlockSpec((tm, d), lambda i: (i, 0)),
            pl.BlockSpec((1, d), lambda i: (0, 0)),
            mod, mod,
        ],
        out_specs=pl.BlockSpec((tm, d), lambda i: (i, 0)),
        out_shape=jax.ShapeDtypeStruct((m, d), BF16),
        compiler_params=_cparams("parallel"),
        name="prenorm",
    )(x, g.reshape(1, d), sc, sh)


def _final_norm_kernel(x_ref, g_ref, o_ref):
    o_ref[...] = _rms(x_ref[...]) * g_ref[...]


def _final_norm(x, g):
    m, d = x.shape
    tm = min(TM_ELEMWISE, m)
    return pl.pallas_call(
        _final_norm_kernel,
        grid=(m // tm,),
        in_specs=[pl.BlockSpec((tm, d), lambda i: (i, 0)), pl.BlockSpec((1, d), lambda i: (0, 0))],
        out_specs=pl.BlockSpec((tm, d), lambda i: (i, 0)),
        out_shape=jax.ShapeDtypeStruct((m, d), F32),
        compiler_params=_cparams("parallel"),
        name="final_norm",
    )(x, g.reshape(1, d))


def _proj_kernel(*refs, rope, half, scale, act, n_plain, transposed, aliased):
    a_ref, w_ref = refs[:2]
    pos = 2
    acc = jnp.dot(a_ref[...], w_ref[...].astype(BF16), preferred_element_type=F32)
    if rope:
        ca_ref, cb_ref, cc_ref = refs[pos:pos + 3]
        pos += 3
        tn = acc.shape[1]
        reps = tn // LANES
        ca = jnp.tile(ca_ref[...], (1, reps))
        cb = jnp.tile(cb_ref[...], (1, reps))
        cc = jnp.tile(cc_ref[...], (1, reps))
        acc = acc * ca + pltpu.roll(acc, half, 1) * cb + pltpu.roll(acc, tn - half, 1) * cc
    if aliased:
        pos += 1
    if scale is not None:
        acc = acc * scale
    if act == "sigmoid":
        acc = _sigmoid(acc)
    outs = refs[pos:]
    for o in outs[:n_plain]:
        o[...] = acc.astype(o.dtype)
    if transposed:
        outs[n_plain][...] = acc.T.astype(outs[n_plain].dtype)


def _proj(a, w, layer, col0, ncols, out_dtypes, *, rope=None, scale=None, act=None,
          transposed=None, stack=None):
    m, k = a.shape
    tm = min(TM_MATMUL, m)
    tn = TN
    assert col0 % tn == 0 and ncols % tn == 0 and m % tm == 0
    cb0 = col0 // tn
    in_specs = [
        pl.BlockSpec((tm, k), lambda i, j: (i, 0)),
        pl.BlockSpec((None, k, tn), lambda i, j: (layer, 0, cb0 + j)),
    ]
    args = [a, w]
    half = None
    if rope is not None:
        tabs, half = rope
        for t in tabs:
            in_specs.append(pl.BlockSpec((tm, LANES), lambda i, j: (i, 0)))
            args.append(t)
    aliases = {}
    out_specs, out_shape = [], []
    for n, dt in enumerate(out_dtypes):
        if n == 0 and stack is not None:
            assert stack.shape[1:] == (m, ncols) and stack.dtype == dt
            out_specs.append(pl.BlockSpec((None, tm, tn), lambda i, j: (layer, i, j)))
            out_shape.append(jax.ShapeDtypeStruct(stack.shape, dt))
            aliases[len(args)] = 0
            in_specs.append(pl.BlockSpec(memory_space=pl.ANY))
            args.append(stack)
        else:
            out_specs.append(pl.BlockSpec((tm, tn), lambda i, j: (i, j)))
            out_shape.append(jax.ShapeDtypeStruct((m, ncols), dt))
    if transposed is not None:
        out_specs.append(pl.BlockSpec((tn, tm), lambda i, j: (j, i)))
        out_shape.append(jax.ShapeDtypeStruct((ncols, m), transposed))
    return pl.pallas_call(
        functools.partial(_proj_kernel, rope=rope is not None, half=half, scale=scale, act=act,
                          n_plain=len(out_dtypes), transposed=transposed is not None,
                          aliased=bool(aliases)),
        grid=(m // tm, ncols // tn),
        in_specs=in_specs,
        out_specs=out_specs,
        out_shape=out_shape,
        input_output_aliases=aliases,
        compiler_params=_cparams("parallel", "parallel"),
        name="proj",
    )(*args)


def _gates_kernel(a_ref, wc_ref, wr_ref, bc_ref, br_ref, oc_ref, or_ref, *, nh):
    a = a_ref[...]
    col = jnp.dot(a, wc_ref[...], preferred_element_type=F32) + bc_ref[...]
    row = lax.dot_general(wr_ref[...], a, _NT, preferred_element_type=F32) + br_ref[...]
    lane = lax.broadcasted_iota(jnp.int32, col.shape, 1)
    oc_ref[...] = jnp.where(lane < nh, col, _log_sigmoid(col))
    sub = lax.broadcasted_iota(jnp.int32, row.shape, 0)
    or_ref[...] = jnp.where(sub < nh, row, _log_sigmoid(row))


def _gates(a, w_if, b_if, nh):
    m, k = a.shape
    tm = min(1024, m)
    ng = 2 * nh
    wc = jnp.zeros((k, LANES), BF16).at[:, :ng].set(w_if.astype(BF16))
    wr = jnp.zeros((2 * SUBLANES, k), BF16).at[:ng, :].set(w_if.T.astype(BF16))
    bc = jnp.zeros((1, LANES), F32).at[0, :ng].set(b_if)
    br = jnp.zeros((2 * SUBLANES, 1), F32).at[:ng, 0].set(b_if)
    return pl.pallas_call(
        functools.partial(_gates_kernel, nh=nh),
        grid=(m // tm,),
        in_specs=[
            pl.BlockSpec((tm, k), lambda i: (i, 0)),
            pl.BlockSpec((k, LANES), lambda i: (0, 0)),
            pl.BlockSpec((2 * SUBLANES, k), lambda i: (0, 0)),
            pl.BlockSpec((1, LANES), lambda i: (0, 0)),
            pl.BlockSpec((2 * SUBLANES, 1), lambda i: (0, 0)),
        ],
        out_specs=[
            pl.BlockSpec((tm, LANES), lambda i: (i, 0)),
            pl.BlockSpec((2 * SUBLANES, tm), lambda i: (0, i)),
        ],
        out_shape=[
            jax.ShapeDtypeStruct((m, LANES), F32),
            jax.ShapeDtypeStruct((2 * SUBLANES, m), F32),
        ],
        compiler_params=_cparams("parallel"),
        name="gates",
    )(a, wc, wr, bc, br)


def _conv_kernel(x_ref, prev_ref, c0_ref, w_ref, b_ref, sc_ref, o_ref, buf, *, ts, cw):
    first = pl.program_id(1) == 0
    buf[0:SUBLANES, :] = jnp.where(first, c0_ref[0], prev_ref[0])
    buf[SUBLANES:SUBLANES + ts, :] = x_ref[0]
    acc = b_ref[...]
    for j in range(cw):
        lo = SUBLANES - (cw - 1) + j
        acc = acc + w_ref[j:j + 1, :] * buf[lo:lo + ts, :]
    o_ref[0] = (_silu(acc) * sc_ref[...]).astype(o_ref.dtype)


def _conv_silu(x, conv0, conv_w, conv_b, out_scale, ts):
    b, s, c = x.shape
    cw = conv_w.shape[0]
    c0 = jnp.zeros((b, SUBLANES, c), F32).at[:, SUBLANES - (cw - 1):, :].set(conv0)
    wpad = jnp.zeros((SUBLANES, c), F32).at[:cw].set(conv_w)
    rb = ts // SUBLANES
    return pl.pallas_call(
        functools.partial(_conv_kernel, ts=ts, cw=cw),
        grid=(b, s // ts),
        in_specs=[
            pl.BlockSpec((1, ts, c), lambda bi, i: (bi, i, 0)),
            pl.BlockSpec((1, SUBLANES, c), lambda bi, i: (bi, jnp.maximum(i * rb - 1, 0), 0)),
            pl.BlockSpec((1, SUBLANES, c), lambda bi, i: (bi, 0, 0)),
            pl.BlockSpec((SUBLANES, c), lambda bi, i: (0, 0)),
            pl.BlockSpec((1, c), lambda bi, i: (0, 0)),
            pl.BlockSpec((1, c), lambda bi, i: (0, 0)),
        ],
        out_specs=pl.BlockSpec((1, ts, c), lambda bi, i: (bi, i, 0)),
        out_shape=jax.ShapeDtypeStruct((b, s, c), BF16),
        scratch_shapes=[pltpu.VMEM((ts + SUBLANES, c), F32)],
        compiler_params=_cparams("parallel", "parallel"),
        name="conv_silu",
    )(x, x, c0, wpad, conv_b.reshape(1, c), out_scale.reshape(1, c))


def _mlstm_kernel(qk_ref, v_ref, gc_ref, gr_ref, so_ref, c0_ref, n0_ref, m0_ref, gm_ref,
                  y_ref, c1_ref, n1_ref, m1_ref, cs, ms, *, nh, dk, dv, chunk):
    c = pl.program_id(1)
    nc = pl.num_programs(1)
    ext = dv + LANES
    L = chunk

    @pl.when(c == 0)
    def _():
        for h in range(nh):
            cs[h, 0:dv, :] = c0_ref[0, h]
            cs[h, dv:dv + SUBLANES, :] = n0_ref[0, h]
            cs[h, dv + SUBLANES:ext, :] = jnp.zeros((LANES - SUBLANES, dk), F32)
        ms[...] = m0_ref[0]

    row_t = lax.broadcasted_iota(jnp.int32, (L, L), 0)
    col_s = lax.broadcasted_iota(jnp.int32, (L, L), 1)
    causal = col_s <= row_t
    upper = col_s >= row_t
    gr = gr_ref[0]
    gc = gc_ref[0]
    one_col = (lax.broadcasted_iota(jnp.int32, (L, LANES), 1) == 0).astype(BF16)

    for h in range(nh):
        q = qk_ref[0, :, h * dk:(h + 1) * dk]
        k = qk_ref[0, :, nh * dk + h * dk:nh * dk + (h + 1) * dk]
        v = v_ref[0, :, h * dv:(h + 1) * dv]
        ig_row = gr[h:h + 1, :]
        lf_row = gr[nh + h:nh + h + 1, :]
        ig_col = gc[:, h:h + 1]
        lf_col = gc[:, nh + h:nh + h + 1]
        m_prev = ms[...][h:h + 1, 0:1]

        b_col = jnp.sum(jnp.where(causal, lf_row, 0.0), axis=1, keepdims=True)
        b_row = jnp.sum(jnp.where(upper, lf_col, 0.0), axis=0, keepdims=True)
        a_row = ig_row - b_row
        a_col = ig_col - b_col
        g_col = jnp.maximum(jnp.max(jnp.where(causal, a_row, NEG), axis=1, keepdims=True), m_prev)
        d = jnp.exp(jnp.where(causal, a_row - g_col, NEG))
        s = lax.dot_general(q, k, _NT, preferred_element_type=F32) * d
        v_ext = jnp.concatenate([v, one_col], axis=1)
        state = cs[h]
        r1 = jnp.dot(s.astype(BF16), v_ext, preferred_element_type=F32)
        r2 = lax.dot_general(q, state.astype(BF16), _NT, preferred_element_type=F32)
        tot = r1 + jnp.exp(m_prev - g_col) * r2
        num = tot[:, 0:dv]
        den = tot[:, dv:dv + 1]
        hh = num / jnp.maximum(jnp.abs(den), jnp.exp(-(b_col + g_col)))
        gate = so_ref[0, :, h * dv:(h + 1) * dv].astype(F32)
        y = _rms(hh) * gm_ref[:, h * dv:(h + 1) * dv] * gate
        y_ref[0, :, h * dv:(h + 1) * dv] = y.astype(y_ref.dtype)

        g_last = g_col[L - 1:L, :]
        vw = (v_ext.astype(F32) * jnp.exp(a_col - g_last)).astype(BF16)
        upd = lax.dot_general(vw, k, _TN, preferred_element_type=F32)
        cs[h] = jnp.exp(m_prev - g_last) * state + upd
        ms[h:h + 1, :] = jnp.broadcast_to(b_col[L - 1:L, :] + g_last, (1, LANES))

    @pl.when(c == nc - 1)
    def _():
        for h in range(nh):
            c1_ref[0, h] = cs[h, 0:dv, :]
            n1_ref[0, h] = cs[h, dv:dv + SUBLANES, :]
        m1_ref[0] = ms[...]


def _mlstm(qk, v, gcol, grow, so, c0, n0, m0, g_mnorm, chunk):
    b, s, _ = qk.shape
    _, nh, dv, dk = c0.shape
    nc = s // chunk
    n0p = jnp.zeros((b, nh, SUBLANES, dk), F32).at[:, :, 0, :].set(n0)
    m0p = jnp.zeros((b, SUBLANES, LANES), F32).at[:, :nh, :].set(
        jnp.broadcast_to(m0[:, :, None], (b, nh, LANES)))
    y, c1, n1, m1 = pl.pallas_call(
        functools.partial(_mlstm_kernel, nh=nh, dk=dk, dv=dv, chunk=chunk),
        grid=(b, nc),
        in_specs=[
            pl.BlockSpec((1, chunk, 2 * nh * dk), lambda bi, ci: (bi, ci, 0)),
            pl.BlockSpec((1, chunk, nh * dv), lambda bi, ci: (bi, ci, 0)),
            pl.BlockSpec((1, chunk, LANES), lambda bi, ci: (bi, ci, 0)),
            pl.BlockSpec((1, 2 * SUBLANES, chunk), lambda bi, ci: (bi, 0, ci)),
            pl.BlockSpec((1, chunk, nh * dv), lambda bi, ci: (bi, ci, 0)),
            pl.BlockSpec((1, nh, dv, dk), lambda bi, ci: (bi, 0, 0, 0)),
            pl.BlockSpec((1, nh, SUBLANES, dk), lambda bi, ci: (bi, 0, 0, 0)),
            pl.BlockSpec((1, SUBLANES, LANES), lambda bi, ci: (bi, 0, 0)),
            pl.BlockSpec((1, nh * dv), lambda bi, ci: (0, 0)),
        ],
        out_specs=[
            pl.BlockSpec((1, chunk, nh * dv), lambda bi, ci: (bi, ci, 0)),
            pl.BlockSpec((1, nh, dv, dk), lambda bi, ci: (bi, 0, 0, 0)),
            pl.BlockSpec((1, nh, SUBLANES, dk), lambda bi, ci: (bi, 0, 0, 0)),
            pl.BlockSpec((1, SUBLANES, LANES), lambda bi, ci: (bi, 0, 0)),
        ],
        out_shape=[
            jax.ShapeDtypeStruct((b, s, nh * dv), BF16),
            jax.ShapeDtypeStruct((b, nh, dv, dk), F32),
            jax.ShapeDtypeStruct((b, nh, SUBLANES, dk), F32),
            jax.ShapeDtypeStruct((b, SUBLANES, LANES), F32),
        ],
        scratch_shapes=[pltpu.VMEM((nh, dv + LANES, dk), F32), pltpu.VMEM((SUBLANES, LANES), F32)],
        compiler_params=_cparams("parallel", "arbitrary"),
        name="mlstm",
    )(qk, v, gcol, grow, so, c0, n0p, m0p, g_mnorm.reshape(1, nh * dv))
    return y, c1, n1[:, :, 0, :], m1[:, :nh, 0]


def _lam_value(lam_ref, lam_init):
    p = lam_ref[...]
    s01 = jnp.sum(p[0:1] * p[1:2], axis=1, keepdims=True)
    s23 = jnp.sum(p[2:3] * p[3:4], axis=1, keepdims=True)
    return jnp.exp(s01) - jnp.exp(s23) + lam_init


def _flash_kernel(qi_tab, ki_tab, lam_ref, q_ref, k_ref, vt_ref, gd_ref, o_ref, m_s, l_s, acc_s,
                  *, nsub, t, dq, lam_init):
    step = pl.program_id(2)
    qi = qi_tab[step]
    ki = ki_tab[step]
    rel = ki - qi * nsub

    @pl.when(ki == 0)
    def _():
        m_s[...] = jnp.full(m_s.shape, NEG, F32)
        l_s[...] = jnp.zeros(l_s.shape, F32)
        acc_s[...] = jnp.zeros(acc_s.shape, F32)

    def tile_update(qh, masked):
        lo, hi = qh * t, (qh + 1) * t
        if masked:
            keep = (lax.broadcasted_iota(jnp.int32, (t, t), 0)
                    <= lax.broadcasted_iota(jnp.int32, (t, t), 1))
        for mi in range(2):
            km = k_ref[0, :, mi * dq:(mi + 1) * dq]
            qm = q_ref[0, lo:hi, mi * dq:(mi + 1) * dq]
            st = lax.dot_general(km, qm, _NT, preferred_element_type=F32)
            if masked:
                st = jnp.where(keep, st, NEG)
            m_prev = m_s[mi, :, lo:hi]
            m_new = jnp.maximum(m_prev, jnp.max(st, axis=0, keepdims=True))
            alpha = jnp.exp2(m_prev - m_new)
            p = jnp.exp2(st - m_new)
            l_s[mi, :, lo:hi] = alpha * l_s[mi, :, lo:hi] + jnp.sum(p, axis=0, keepdims=True)
            acc_s[mi, :, lo:hi] = alpha * acc_s[mi, :, lo:hi] + jnp.dot(
                vt_ref[...], p.astype(BF16), preferred_element_type=F32)
            m_s[mi, :, lo:hi] = m_new

    @pl.when(rel < 0)
    def _():
        for qh in range(nsub):
            tile_update(qh, False)

    for j in range(nsub):
        @pl.when(rel == j)
        def _(j=j):
            for qh in range(j, nsub):
                tile_update(qh, qh == j)

    @pl.when(rel == nsub - 1)
    def _():
        lam = _lam_value(lam_ref, lam_init)
        ot = acc_s[0] / l_s[0] - lam * (acc_s[1] / l_s[1])
        o = _rms(ot, axis=0).T
        o_ref[0] = (o * gd_ref[0] * (1.0 - lam_init)).astype(o_ref.dtype)


def _flash(q, k, vt, lam_p, g_dnorm, lam_init):
    b, s, w = q.shape
    nh, dv = g_dnorm.shape
    dq = dv // 2
    t = min(ATT_T, s)
    nsub = min(ATT_NSUB, s // t)
    tq = t * nsub
    nq, nkb = s // tq, s // t
    steps = [(qi, ki) for qi in range(nq) for ki in range((qi + 1) * nsub)]
    qi_tab = jnp.asarray(np.array([p[0] for p in steps], np.int32))
    ki_tab = jnp.asarray(np.array([p[1] for p in steps], np.int32))

    grid_spec = pltpu.PrefetchScalarGridSpec(
        num_scalar_prefetch=2,
        grid=(b, nh, len(steps)),
        in_specs=[
            pl.BlockSpec(lam_p.shape, lambda bi, h, st, qt, kt: (0, 0)),
            pl.BlockSpec((1, tq, dv), lambda bi, h, st, qt, kt: (bi, qt[st], h)),
            pl.BlockSpec((1, t, dv), lambda bi, h, st, qt, kt: (bi, kt[st], h)),
            pl.BlockSpec((dv, t), lambda bi, h, st, qt, kt: (h, bi * nkb + kt[st])),
            pl.BlockSpec((1, 1, dv), lambda bi, h, st, qt, kt: (h, 0, 0)),
        ],
        out_specs=pl.BlockSpec((1, tq, dv), lambda bi, h, st, qt, kt: (bi, qt[st], h)),
        scratch_shapes=[
            pltpu.VMEM((2, 1, tq), F32),
            pltpu.VMEM((2, 1, tq), F32),
            pltpu.VMEM((2, dv, tq), F32),
        ],
    )
    return pl.pallas_call(
        functools.partial(_flash_kernel, nsub=nsub, t=t, dq=dq, lam_init=lam_init),
        grid_spec=grid_spec,
        out_shape=jax.ShapeDtypeStruct((b, s, w), BF16),
        compiler_params=_cparams("parallel", "parallel", "arbitrary"),
        name="flash_diff",
    )(qi_tab, ki_tab, lam_p, q, k, vt, g_dnorm.reshape(nh, 1, dv))


def _decode_kernel(pt_ref, lam_ref, q_ref, bias_ref, *rest, npg, nh, ds, dv, lam_init):
    k_refs = rest[:npg]
    v_refs = rest[npg:2 * npg]
    kn_ref, vn_ref, biasn_ref, gd_ref, o_ref, m_s, l_s, acc_s = rest[2 * npg:]
    j = pl.program_id(1)
    nj = pl.num_programs(1)

    @pl.when(j == 0)
    def _():
        m_s[...] = jnp.full(m_s.shape, NEG, F32)
        l_s[...] = jnp.zeros(l_s.shape, F32)
        acc_s[...] = jnp.zeros(acc_s.shape, F32)

    q = q_ref[0]

    def update(kvb):
        scores = [lax.dot_general(q, kk, _NT, preferred_element_type=F32) + bias for kk, _, bias in kvb]
        m_prev = m_s[...]
        m_new = m_prev
        for s in scores:
            m_new = jnp.maximum(m_new, jnp.max(s, axis=1, keepdims=True))
        alpha = jnp.exp2(m_prev - m_new)
        l_new = alpha * l_s[...]
        acc = alpha * acc_s[...]
        for s, (_, vv, _) in zip(scores, kvb):
            p = jnp.exp2(s - m_new)
            l_new = l_new + jnp.sum(p, axis=1, keepdims=True)
            acc = acc + jnp.dot(p.astype(BF16), vv, preferred_element_type=F32)
        l_s[...] = l_new
        acc_s[...] = acc
        m_s[...] = m_new

    pages = []
    for i in range(npg):
        kp = k_refs[i][...]
        vp = v_refs[i][...]
        rows = kp.shape[0] * kp.shape[1]
        pages.append((kp.reshape(rows, dv).astype(BF16), vp.reshape(rows, dv).astype(BF16),
                      bias_ref[...]))
    update(pages)

    @pl.when(j == nj - 1)
    def _():
        update([(kn_ref[0], vn_ref[0], biasn_ref[...])])
        lam = _lam_value(lam_ref, lam_init)
        o_all = (acc_s[...] / l_s[...]).reshape(nh, 2, ds, dv)
        o = o_all[:, 0] - lam * o_all[:, 1]
        o_ref[0] = (_rms(o) * gd_ref[...] * (1.0 - lam_init)).astype(o_ref.dtype)


def _decode_attn(q, k_new, v_new, cache_k, cache_v, layer, page_table, lam_p, g_dnorm, lam_init, npg=4):
    db, ds, w = q.shape
    nh, dv = g_dnorm.shape
    dq = dv // 2
    page = cache_k.shape[2]
    n_pages = page_table.shape[1]
    assert n_pages % npg == 0
    nrow = nh * 2 * ds

    q5 = q.reshape(db, ds, nh, 2, dq).transpose(0, 2, 3, 1, 4)
    zero = jnp.zeros_like(q5[:, :, 0])
    qbd = jnp.stack([jnp.concatenate([q5[:, :, 0], zero], -1),
                     jnp.concatenate([zero, q5[:, :, 1]], -1)], axis=2).reshape(db, nrow, dv)

    r_head = (jnp.arange(nrow) // (2 * ds))[:, None]
    r_tok = (jnp.arange(nrow) % ds)[:, None]
    c_head = (jnp.arange(page * nh) % nh)[None, :]
    bias = jnp.where(c_head == r_head, 0.0, NEG).astype(F32)
    cn_head = (jnp.arange(ds * nh) % nh)[None, :]
    cn_tok = (jnp.arange(ds * nh) // nh)[None, :]
    bias_new = jnp.where((cn_head == r_head) & (cn_tok <= r_tok), 0.0, NEG).astype(F32)

    kn = k_new.reshape(db, ds * nh, dv)
    vn = v_new.reshape(db, ds * nh, dv)
    pt = page_table.reshape(-1).astype(jnp.int32)

    def page_map(i):
        return lambda bi, j, pt_ref: (layer, pt_ref[bi * n_pages + j * npg + i], 0, 0, 0)

    page_spec = [pl.BlockSpec((None, None, page, nh, dv), page_map(i)) for i in range(npg)]
    grid_spec = pltpu.PrefetchScalarGridSpec(
        num_scalar_prefetch=1,
        grid=(db, n_pages // npg),
        in_specs=[
            pl.BlockSpec(lam_p.shape, lambda bi, j, pt_ref: (0, 0)),
            pl.BlockSpec((1, nrow, dv), lambda bi, j, pt_ref: (bi, 0, 0)),
            pl.BlockSpec(bias.shape, lambda bi, j, pt_ref: (0, 0)),
            *page_spec, *page_spec,
            pl.BlockSpec((1, ds * nh, dv), lambda bi, j, pt_ref: (bi, 0, 0)),
            pl.BlockSpec((1, ds * nh, dv), lambda bi, j, pt_ref: (bi, 0, 0)),
            pl.BlockSpec(bias_new.shape, lambda bi, j, pt_ref: (0, 0)),
            pl.BlockSpec((nh, 1, dv), lambda bi, j, pt_ref: (0, 0, 0)),
        ],
        out_specs=pl.BlockSpec((1, nh, ds, dv), lambda bi, j, pt_ref: (bi, 0, 0, 0)),
        scratch_shapes=[
            pltpu.VMEM((nrow, 1), F32),
            pltpu.VMEM((nrow, 1), F32),
            pltpu.VMEM((nrow, dv), F32),
        ],
    )
    o = pl.pallas_call(
        functools.partial(_decode_kernel, npg=npg, nh=nh, ds=ds, dv=dv, lam_init=lam_init),
        grid_spec=grid_spec,
        out_shape=jax.ShapeDtypeStruct((db, nh, ds, dv), BF16),
        compiler_params=_cparams("parallel", "arbitrary"),
        name="decode_attn",
    )(pt, lam_p, qbd, bias, *([cache_k] * npg), *([cache_v] * npg), kn, vn, bias_new,
      g_dnorm.reshape(nh, 1, dv))
    return o.transpose(0, 2, 1, 3).reshape(db, ds, w)


def _merge_kernel(a_ref, b_ref, wa_ref, wb_ref, sga_ref, sgb_ref, o_ref):
    ya = jnp.dot(a_ref[...], wa_ref[...], preferred_element_type=F32)
    yb = jnp.dot(b_ref[...], wb_ref[...], preferred_element_type=F32)
    o_ref[...] = (sga_ref[...].astype(F32) * ya + sgb_ref[...].astype(F32) * yb).astype(o_ref.dtype)


def _merge(a, b, wa, wb, layer, sg):
    m, k = a.shape
    n = wa.shape[2]
    tm = min(TM_TWO_OPERAND, m)
    tn = TN
    nb = n // tn
    return pl.pallas_call(
        _merge_kernel,
        grid=(m // tm, nb),
        in_specs=[
            pl.BlockSpec((tm, k), lambda i, j: (i, 0)),
            pl.BlockSpec((tm, k), lambda i, j: (i, 0)),
            pl.BlockSpec((None, k, tn), lambda i, j: (layer, 0, j)),
            pl.BlockSpec((None, k, tn), lambda i, j: (layer, 0, j)),
            pl.BlockSpec((tm, tn), lambda i, j: (i, j)),
            pl.BlockSpec((tm, tn), lambda i, j: (i, nb + j)),
        ],
        out_specs=pl.BlockSpec((tm, tn), lambda i, j: (i, j)),
        out_shape=jax.ShapeDtypeStruct((m, n), BF16),
        compiler_params=_cparams("parallel", "parallel"),
        name="merge",
    )(a, b, wa, wb, sg, sg)


def _resid_kernel(a_ref, w_ref, x_ref, gt_ref, o_ref):
    y = jnp.dot(a_ref[...], w_ref[...], preferred_element_type=F32)
    o_ref[...] = x_ref[...] + gt_ref[0] * y


def _resid_matmul(a, w, layer, x, gt, rpg, tm_pref, tn):
    m, k = a.shape
    n = w.shape[2]
    tm, tpg = _row_tile(m, rpg, tm_pref)
    r = gt.shape[1]
    assert r in (1, tm)
    return pl.pallas_call(
        _resid_kernel,
        grid=(m // tm, n // tn),
        in_specs=[
            pl.BlockSpec((tm, k), lambda i, j: (i, 0)),
            pl.BlockSpec((None, k, tn), lambda i, j: (layer, 0, j)),
            pl.BlockSpec((tm, tn), lambda i, j: (i, j)),
            pl.BlockSpec((1, r, tn), lambda i, j: (i // tpg, 0, j)),
        ],
        out_specs=pl.BlockSpec((tm, tn), lambda i, j: (i, j)),
        out_shape=jax.ShapeDtypeStruct((m, n), F32),
        compiler_params=_cparams("parallel", "parallel"),
        name="resid_matmul",
    )(a, w, x, gt)


def _swiglu_kernel(a_ref, wg_ref, wu_ref, o_ref):
    a = a_ref[...]
    g = jnp.dot(a, wg_ref[...].astype(BF16), preferred_element_type=F32)
    u = jnp.dot(a, wu_ref[...].astype(BF16), preferred_element_type=F32)
    o_ref[...] = (_silu(g) * u).astype(o_ref.dtype)


def _swiglu(a, w_gu, layer):
    m, k = a.shape
    dff = w_gu.shape[2] // 2
    tm = min(TM_MATMUL, m)
    tn = TN_SWIGLU
    nb = dff // tn
    return pl.pallas_call(
        _swiglu_kernel,
        grid=(m // tm, nb),
        in_specs=[
            pl.BlockSpec((tm, k), lambda i, j: (i, 0)),
            pl.BlockSpec((None, k, tn), lambda i, j: (layer, 0, j)),
            pl.BlockSpec((None, k, tn), lambda i, j: (layer, 0, nb + j)),
        ],
        out_specs=pl.BlockSpec((tm, tn), lambda i, j: (i, j)),
        out_shape=jax.ShapeDtypeStruct((m, dff), BF16),
        compiler_params=_cparams("parallel", "parallel"),
        name="swiglu",
    )(a, w_gu, w_gu)


def _rope_tables(pos, dq, reps):
    rot = dq // 4
    half = rot // 2
    inv = jnp.power(ROPE_THETA, -jnp.arange(0, rot, 2, dtype=F32) / rot)
    ang = pos.astype(F32)[:, None] * inv[None, :]
    cos, sin = jnp.cos(ang), jnp.sin(ang)
    n = pos.shape[0]
    one = jnp.ones((n, dq - rot), F32)
    zero_h = jnp.zeros((n, half), F32)
    zero_r = jnp.zeros((n, dq - rot), F32)
    ta = jnp.concatenate([cos, cos, one], axis=1)
    tb = jnp.concatenate([zero_h, sin, zero_r], axis=1)
    tc = jnp.concatenate([-sin, zero_h, zero_r], axis=1)
    return [jnp.tile(t, (reps, 1)) for t in (ta, tb, tc)], half


def _layer(x, mods, wts, st, geom, layer, past, kv_bufs):
    b, s, rpg, chunk = geom
    m, d = x.shape
    sh1, sc1, gt1, sh2, sc2, gt2 = mods
    nh_a, dv_a, dk_a = st["C"].shape[1:]
    nh_b, dv_b = wts["g_dnorm"].shape[1:]
    dq_b = dv_b // 2
    w_qk_a, w_v_a = nh_a * dk_a, nh_a * dv_a
    w_b = nh_b * dv_b
    lam_init = 0.8 - 0.6 * math.exp(-0.3 * layer)

    h1 = _prenorm(x, wts["g_norm1"][layer], sc1, sh1, rpg)

    w_in, w_in_b = wts["w_in"], wts["w_in_b"]
    (qk_pre,) = _proj(h1, w_in, layer, 0, 2 * w_qk_a, (F32,))
    (v_a,) = _proj(h1, w_in, layer, 2 * w_qk_a, w_v_a, (BF16,))
    (so_a,) = _proj(h1, w_in, layer, 2 * w_qk_a + w_v_a, w_v_a, (BF16,), act="sigmoid")
    gcol, grow = _gates(h1, wts["w_if"][layer], wts["b_if"][layer], nh_a)

    rope = _rope_tables(st["pos"], dq_b, b)
    (q_b,) = _proj(h1, w_in_b, layer, 0, w_b, (BF16,), rope=rope, scale=dq_b ** -0.5 * LOG2E)
    k_buf, v_buf = kv_bufs
    k_st, k_h = _proj(h1, w_in_b, layer, w_b, w_b, (F32, BF16), rope=rope, stack=k_buf)
    if past is None:
        v_st, v_h, v_t = _proj(h1, w_in_b, layer, 2 * w_b, w_b, (F32, BF16), transposed=BF16,
                               stack=v_buf)
    else:
        v_st, v_h = _proj(h1, w_in_b, layer, 2 * w_b, w_b, (F32, BF16), stack=v_buf)
    (sg,) = _proj(h1, w_in_b, layer, 3 * w_b, 2 * d, (BF16,), act="sigmoid")

    qk_pre3 = qk_pre.reshape(b, s, 2 * w_qk_a)
    k_scale = jnp.concatenate([jnp.ones((w_qk_a,), F32), jnp.full((w_qk_a,), dk_a ** -0.5, F32)])
    qk_a = _conv_silu(qk_pre3, st["conv"], wts["conv_w"][layer], wts["conv_b"][layer], k_scale,
                      min(s, 256))
    grow3 = grow.reshape(2 * SUBLANES, b, s).transpose(1, 0, 2)
    y_a, c1, n1, m1 = _mlstm(qk_a, v_a.reshape(b, s, w_v_a), gcol.reshape(b, s, LANES), grow3,
                             so_a.reshape(b, s, w_v_a), st["C"], st["n"], st["m"],
                             wts["g_mnorm"][layer], chunk)
    cw = wts["conv_w"].shape[1]
    assert s >= cw - 1
    conv1 = qk_pre3[:, s - (cw - 1):]

    if past is None:
        y_b = _flash(q_b.reshape(b, s, w_b), k_h.reshape(b, s, w_b), v_t,
                     wts["lam"][layer], wts["g_dnorm"][layer], lam_init)
    else:
        cache_k, cache_v, page_table = past
        y_b = _decode_attn(q_b.reshape(b, s, w_b), k_h.reshape(b, s, nh_b, dv_b),
                           v_h.reshape(b, s, nh_b, dv_b), cache_k, cache_v, layer, page_table,
                           wts["lam"][layer], wts["g_dnorm"][layer], lam_init)

    mix = _merge(y_a.reshape(m, w_v_a), y_b.reshape(m, w_b), wts["w_branch_a"], wts["w_branch_b"],
                 layer, sg)
    x = _resid_matmul(mix, wts["w_out"], layer, x, gt1, rpg, TM_TWO_OPERAND, TN)

    h2 = _prenorm(x, wts["g_norm2"][layer], sc2, sh2, rpg)
    hid = _swiglu(h2, wts["w_gate_up"], layer)
    x = _resid_matmul(hid, wts["w_down"], layer, x, gt2, rpg, TM_TWO_OPERAND, TN_DOWN)

    return x, (k_st, v_st), (c1, n1, m1, conv1)


def kernel(x_prompt, x_sample, c_prompt, c_sample, cache_k, cache_v, page_table, state_C, state_n,
           state_m, state_conv, g_norm1, g_norm2, w_ada, b_ada, w_in, conv_w, conv_b, b_if, g_mnorm,
           lam, g_dnorm, w_branch_a, w_branch_b, w_out, w_gate_up, w_down, g_final):
    bp, sp, d = x_prompt.shape
    db, ds, _ = x_sample.shape
    depth = w_in.shape[0]
    nh_a, dv_a, dk_a = state_C.shape[2:]
    nh_b, dv_b = g_dnorm.shape[1:]
    n_ada = w_ada.shape[2] // d
    past_len = page_table.shape[1] * cache_k.shape[2]
    w_a_cols = 2 * nh_a * dk_a + 2 * nh_a * dv_a
    n_if = 2 * nh_a

    rows = 2 * SUBLANES
    assert bp + db <= rows
    c_all = jnp.zeros((rows, d), F32).at[:bp].set(c_prompt).at[bp:bp + db].set(c_sample)
    ada = _ada(c_all, w_ada, b_ada)

    wts = {
        "g_norm1": g_norm1, "g_norm2": g_norm2, "w_in": w_in,
        "w_if": w_in[:, :, w_a_cols:w_a_cols + n_if],
        "w_in_b": w_in[:, :, w_a_cols + n_if:].astype(BF16),
        "b_if": b_if, "conv_w": conv_w, "conv_b": conv_b, "g_mnorm": g_mnorm,
        "lam": lam, "g_dnorm": g_dnorm,
        "w_branch_a": w_branch_a.astype(BF16), "w_branch_b": w_branch_b.astype(BF16),
        "w_out": w_out.astype(BF16), "w_gate_up": w_gate_up, "w_down": w_down.astype(BF16),
    }

    xp = x_prompt.reshape(bp * sp, d)
    xs = x_sample.reshape(db * ds, d)
    geom_p = (bp, sp, sp, min(MLSTM_CHUNK, sp))
    geom_s = (db, ds, db * ds, ds)
    pos_p = jnp.arange(sp, dtype=jnp.int32)
    pos_s = past_len + jnp.arange(ds, dtype=jnp.int32)

    w_b = nh_b * dv_b
    kv_p = (jnp.zeros((depth, bp * sp, w_b), F32), jnp.zeros((depth, bp * sp, w_b), F32))
    kv_s = (jnp.zeros((depth, db * ds, w_b), F32), jnp.zeros((depth, db * ds, w_b), F32))
    st_outs_p, st_outs_s = [], []
    for l in range(depth):
        a_l = ada[l].reshape(rows, n_ada, d)
        mods_p = [a_l[:bp, i][:, None, :] for i in range(n_ada)]
        mods_s = [jnp.repeat(a_l[bp:bp + db, i], ds, axis=0)[None] for i in range(n_ada)]

        st_p = {
            "C": jnp.zeros((bp, nh_a, dv_a, dk_a), F32), "n": jnp.zeros((bp, nh_a, dk_a), F32),
            "m": jnp.zeros((bp, nh_a), F32), "conv": jnp.zeros((bp,) + state_conv.shape[2:], F32),
            "pos": pos_p,
        }
        xp, kv_p, o_p = _layer(xp, mods_p, wts, st_p, geom_p, l, None, kv_p)
        st_s = {"C": state_C[l], "n": state_n[l], "m": state_m[l], "conv": state_conv[l], "pos": pos_s}
        xs, kv_s, o_s = _layer(xs, mods_s, wts, st_s, geom_s, l, (cache_k, cache_v, page_table), kv_s)
        st_outs_p.append(o_p)
        st_outs_s.append(o_s)

    kp = kv_p[0].reshape(depth, bp, sp, nh_b, dv_b)
    vp = kv_p[1].reshape(depth, bp, sp, nh_b, dv_b)
    k_s = kv_s[0].reshape(depth, db, ds, nh_b, dv_b)
    v_s = kv_s[1].reshape(depth, db, ds, nh_b, dv_b)
    cp, n_p, mp, convp = [jnp.stack(t) for t in zip(*st_outs_p)]
    c_s, n_s, m_s, conv_s = [jnp.stack(t) for t in zip(*st_outs_s)]
    y_prompt = _final_norm(xp, g_final).reshape(bp, sp, d)
    y_sample = _final_norm(xs, g_final).reshape(db, ds, d)
    return (y_prompt, y_sample, kp, vp, cp, n_p, mp, convp, k_s, v_s, c_s, n_s, m_s, conv_s)
```
